```python
import math
import jax, jax.numpy as jnp
from jax import lax
import numpy as np

D_MODEL = 1024
BATCH = 2
SEQ = 8192
DEPTH = 2
DEC_BATCH = 128
DEC_SEQ = 8
PAST_LEN = 8192
PAGE_SIZE = 128

N_META = 16
N_MIXERS = 2
N_LAYERS_A = (DEPTH + 1) // 2
N_LAYERS_B = DEPTH // 2
D_FF = 4 * D_MODEL
RMS_EPS = 1e-6
ROPE_THETA = 10000.0
Q_BLOCK = 128
NEG_INF = -1e30

A_HEAD_DIM = 64
A_HEADS = D_MODEL // (2 * A_HEAD_DIM)
A_KV_HEADS = A_HEADS // 2
A_GROUP = A_HEADS // A_KV_HEADS
A_Q_COLS = A_HEADS * 2 * A_HEAD_DIM
A_K_COLS = A_KV_HEADS * 2 * A_HEAD_DIM
A_V_COLS = A_KV_HEADS * 2 * A_HEAD_DIM
A_LAMBDA_STD = 0.1
A_SCALE = A_HEAD_DIM ** -0.5

B_HEADS = D_MODEL // 128
B_NOPE = 128
B_ROPE = 64
B_V = 128
B_Q_LORA = 3 * D_MODEL // 8
B_KV_LORA = D_MODEL // 4
B_SCALE = (B_NOPE + B_ROPE) ** -0.5

kernel_name = 'diffattn_mla_hybrid_decode_step'


def rmsnorm(x, w):
    xf = x.astype(jnp.float32)
    y = xf * lax.rsqrt(jnp.mean(xf * xf, axis=-1, keepdims=True) + RMS_EPS)
    return (y * w.astype(jnp.float32)).astype(x.dtype)


def rope(x, pos):
    half = x.shape[-1] // 2
    inv_freq = 1.0 / (ROPE_THETA ** (jnp.arange(half, dtype=jnp.float32) / half))
    ang = pos.astype(jnp.float32)[:, None] * inv_freq[None, :]
    bshape = (1, x.shape[1]) + (1,) * (x.ndim - 3) + (half,)
    cos = jnp.cos(ang).reshape(bshape)
    sin = jnp.sin(ang).reshape(bshape)
    xf = x.astype(jnp.float32)
    x1, x2 = xf[..., :half], xf[..., half:]
    return jnp.concatenate([x1 * cos - x2 * sin, x2 * cos + x1 * sin], axis=-1).astype(x.dtype)


def _partial(q, k, v, mask, scale):
    s = jnp.einsum('bqgrmd,bkgmd->bqgrmk', q, k, preferred_element_type=jnp.float32) * scale
    if mask is not None:
        s = jnp.where(mask[None, :, None, None, None, :], s, NEG_INF)
    m = jnp.max(s, axis=-1)
    p = jnp.exp(s - m[..., None])
    o = jnp.einsum('bqgrmk,bkgv->bqgrmv', p, v.astype(jnp.float32))
    return m, jnp.sum(p, axis=-1), o


def _merge(a, b):
    m1, l1, o1 = a
    m2, l2, o2 = b
    m = jnp.maximum(m1, m2)
    c1 = jnp.exp(m1 - m)
    c2 = jnp.exp(m2 - m)
    return m, l1 * c1 + l2 * c2, o1 * c1[..., None] + o2 * c2[..., None]


def prompt_attention(q, k, v, scale):
    b, L = q.shape[:2]
    n_blk = -(-L // Q_BLOCK)
    pad = n_blk * Q_BLOCK - L
    qp = jnp.pad(q, [(0, 0), (0, pad)] + [(0, 0)] * (q.ndim - 2))
    qb = jnp.moveaxis(qp.reshape((b, n_blk, Q_BLOCK) + q.shape[2:]), 1, 0)
    k_pos = jnp.arange(L)

    def block(args):
        q_blk, start = args
        q_pos = start + jnp.arange(Q_BLOCK)
        _, l, o = _partial(q_blk, k, v, k_pos[None, :] <= q_pos[:, None], scale)
        return o / l[..., None]

    ob = jnp.moveaxis(lax.map(block, (qb, jnp.arange(n_blk) * Q_BLOCK)), 0, 1)
    return ob.reshape((b, n_blk * Q_BLOCK) + ob.shape[3:])[:, :L]


def paged_attention(q, k_new, v_new, page_table, page_kv, scale):
    t = q.shape[1]
    causal = jnp.tril(jnp.ones((t, t), dtype=bool))

    def step(carry, phys):
        k_pg, v_pg = page_kv(phys)
        return _merge(carry, _partial(q, k_pg, v_pg, None, scale)), None

    (_, l, o), _ = lax.scan(step, _partial(q, k_new, v_new, causal, scale), page_table.T)
    return o / l[..., None]


def diff_pages(k_pool, v_pool, layer):
    def page_kv(phys):
        k = k_pool[layer, phys]
        return k.reshape(k.shape[:3] + (2, A_HEAD_DIM)), v_pool[layer, phys]
    return page_kv


def mla_pages(c_pool, pe_pool, layer):
    def page_kv(phys):
        c = c_pool[layer, phys]
        k = jnp.concatenate([c, pe_pool[layer, phys]], axis=-1)
        return k[:, :, None, None, :], c[:, :, None, :]
    return page_kv


def diff_qkv(h, pos, w_qkv):
    b, t, _ = h.shape
    q, k, v = jnp.split(h @ w_qkv, [A_Q_COLS, A_Q_COLS + A_K_COLS], axis=-1)
    q = rope(q.reshape(b, t, A_KV_HEADS, A_GROUP, 2, A_HEAD_DIM), pos)
    k = rope(k.reshape(b, t, A_KV_HEADS, 2, A_HEAD_DIM), pos)
    return q, k, v.reshape(b, t, A_KV_HEADS, 2 * A_HEAD_DIM)


def diff_lambda(lam_params, lam_init):
    lq1, lk1, lq2, lk2 = lam_params.astype(jnp.float32)
    return jnp.exp(jnp.sum(lq1 * lk1)) - jnp.exp(jnp.sum(lq2 * lk2)) + lam_init


def diff_combine(o, lam, lam_init, subln, w_o):
    b, t = o.shape[:2]
    a = rmsnorm(o[..., 0, :] - lam * o[..., 1, :], subln) * (1.0 - lam_init)
    return a.reshape(b, t, A_HEADS * 2 * A_HEAD_DIM).astype(w_o.dtype) @ w_o


def mla_project(h, pos, w_a, q_norm, kv_norm, w_uq):
    b, t, _ = h.shape
    c_q, c_kv, k_pe = jnp.split(h @ w_a, [B_Q_LORA, B_Q_LORA + B_KV_LORA], axis=-1)
    q = (rmsnorm(c_q, q_norm) @ w_uq).reshape(b, t, B_HEADS, B_NOPE + B_ROPE)
    return q[..., :B_NOPE], rope(q[..., B_NOPE:], pos), rmsnorm(c_kv, kv_norm), rope(k_pe, pos)


def mla_prompt(h, pos, w_a, q_norm, kv_norm, w_uq, w_ukv, w_o):
    b, t = h.shape[:2]
    q_nope, q_pe, c_kv, k_pe = mla_project(h, pos, w_a, q_norm, kv_norm, w_uq)
    kv = jnp.einsum('btc,chd->bthd', c_kv, w_ukv)
    k_nope, v = kv[..., :B_NOPE], kv[..., B_NOPE:]
    k = jnp.concatenate([k_nope, jnp.broadcast_to(k_pe[:, :, None, :], (b, t, B_HEADS, B_ROPE))], axis=-1)
    q = jnp.concatenate([q_nope, q_pe], axis=-1)
    o = prompt_attention(q[:, :, :, None, None, :], k[:, :, :, None, :], v, B_SCALE)
    y = o.reshape(b, t, B_HEADS * B_V).astype(h.dtype) @ w_o
    return y, c_kv, k_pe


def mla_sample(h, pos, page_table, page_kv, w_a, q_norm, kv_norm, w_uq, w_ukv, w_o):
    b, t = h.shape[:2]
    q_nope, q_pe, c_kv, k_pe = mla_project(h, pos, w_a, q_norm, kv_norm, w_uq)
    q_lat = jnp.einsum('bthn,chn->bthc', q_nope, w_ukv[..., :B_NOPE])
    q = jnp.concatenate([q_lat, q_pe], axis=-1)[:, :, None, :, None, :]
    k_new = jnp.concatenate([c_kv, k_pe], axis=-1)[:, :, None, None, :]
    o_lat = paged_attention(q, k_new, c_kv[:, :, None, :], page_table, page_kv, B_SCALE)
    o = jnp.einsum('bthc,chv->bthv', o_lat[:, :, 0, :, 0, :], w_ukv[..., B_NOPE:])
    y = o.reshape(b, t, B_HEADS * B_V).astype(h.dtype) @ w_o
    return y, c_kv, k_pe


def sq_relu_mlp(h, w_up, w_down):
    u = jax.nn.relu(h @ w_up)
    return (u * u) @ w_down


def setup_inputs(seed: int = 0) -> dict:
    key = jax.random.key(seed)
    ks = jax.random.split(key, 23)
    n_pages = PAST_LEN // PAGE_SIZE
    n_used = DEC_BATCH * n_pages
    n_pool = n_used + n_used // 4

    def nrm(k, shape, scale=1.0):
        return scale * jax.random.normal(k, shape, jnp.float32)

    def gain(k, shape):
        return 1.0 + nrm(k, shape, 0.02)

    page_table = jax.random.permutation(ks[6], n_pool)[:n_used].reshape(DEC_BATCH, n_pages).astype(jnp.int32)
    return {
        'x_prompt': nrm(ks[0], (BATCH, SEQ, D_MODEL)),
        'x_sample': nrm(ks[1], (DEC_BATCH, DEC_SEQ, D_MODEL)),
        'cache_diff_k': nrm(ks[2], (N_LAYERS_A, n_pool, PAGE_SIZE, A_KV_HEADS, 2 * A_HEAD_DIM)),
        'cache_diff_v': nrm(ks[3], (N_LAYERS_A, n_pool, PAGE_SIZE, A_KV_HEADS, 2 * A_HEAD_DIM)),
        'cache_mla_ckv': nrm(ks[4], (N_LAYERS_B, n_pool, PAGE_SIZE, B_KV_LORA)),
        'cache_mla_kpe': nrm(ks[5], (N_LAYERS_B, n_pool, PAGE_SIZE, B_ROPE)),
        'page_table': page_table,
        'meta_tokens': nrm(ks[7], (N_META, D_MODEL)),
        'norm_mix': gain(ks[8], (DEPTH, D_MODEL)),
        'norm_mlp': gain(ks[9], (DEPTH, D_MODEL)),
        'norm_final': gain(ks[10], (D_MODEL,)),
        'a_w_qkv': nrm(ks[11], (N_LAYERS_A, D_MODEL, A_Q_COLS + A_K_COLS + A_V_COLS), D_MODEL ** -0.5),
        'a_lambda': nrm(ks[12], (N_LAYERS_A, 4, A_HEAD_DIM), A_LAMBDA_STD),
        'a_subln': gain(ks[13], (N_LAYERS_A, 2 * A_HEAD_DIM)),
        'a_w_o': nrm(ks[14], (N_LAYERS_A, A_HEADS * 2 * A_HEAD_DIM, D_MODEL), (A_HEADS * 2 * A_HEAD_DIM) ** -0.5),
        'b_w_a': nrm(ks[15], (N_LAYERS_B, D_MODEL, B_Q_LORA + B_KV_LORA + B_ROPE), D_MODEL ** -0.5),
        'b_q_norm': gain(ks[16], (N_LAYERS_B, B_Q_LORA)),
        'b_kv_norm': gain(ks[17], (N_LAYERS_B, B_KV_LORA)),
        'b_w_uq': nrm(ks[18], (N_LAYERS_B, B_Q_LORA, B_HEADS * (B_NOPE + B_ROPE)), B_Q_LORA ** -0.5),
        'b_w_ukv': nrm(ks[19], (N_LAYERS_B, B_KV_LORA, B_HEADS, B_NOPE + B_V), B_KV_LORA ** -0.5),
        'b_w_o': nrm(ks[20], (N_LAYERS_B, B_HEADS * B_V, D_MODEL), (B_HEADS * B_V) ** -0.5),
        'w_up': nrm(ks[21], (DEPTH, D_MODEL, D_FF), D_MODEL ** -0.5),
        'w_down': nrm(ks[22], (DEPTH, D_FF, D_MODEL), D_FF ** -0.5),
    }


def reference(x_prompt, x_sample, cache_diff_k, cache_diff_v, cache_mla_ckv, cache_mla_kpe, page_table,
              meta_tokens, norm_mix, norm_mlp, norm_final, a_w_qkv, a_lambda, a_subln, a_w_o,
              b_w_a, b_q_norm, b_kv_norm, b_w_uq, b_w_ukv, b_w_o, w_up, w_down):
    b = x_prompt.shape[0]
    meta = jnp.broadcast_to(meta_tokens[None].astype(x_prompt.dtype), (b,) + meta_tokens.shape)
    xp = jnp.concatenate([meta, x_prompt], axis=1)
    xs = x_sample
    pos_p = jnp.arange(xp.shape[1])
    pos_s = page_table.shape[1] * PAGE_SIZE + jnp.arange(xs.shape[1])

    dk_p, dv_p, dk_s, dv_s = [], [], [], []
    mc_p, mpe_p, mc_s, mpe_s = [], [], [], []
    for i in range(DEPTH):
        j = i // N_MIXERS
        hp = rmsnorm(xp, norm_mix[i])
        hs = rmsnorm(xs, norm_mix[i])
        if i % N_MIXERS == 0:
            lam_init = 0.8 - 0.6 * math.exp(-0.3 * i)
            lam = diff_lambda(a_lambda[j], lam_init)
            qp, kp, vp = diff_qkv(hp, pos_p, a_w_qkv[j])
            mix_p = diff_combine(prompt_attention(qp, kp, vp, A_SCALE), lam, lam_init, a_subln[j], a_w_o[j])
            qs, kn, vn = diff_qkv(hs, pos_s, a_w_qkv[j])
            o_s = paged_attention(qs, kn, vn, page_table, diff_pages(cache_diff_k, cache_diff_v, j), A_SCALE)
            mix_s = diff_combine(o_s, lam, lam_init, a_subln[j], a_w_o[j])
            dk_p.append(kp.reshape(kp.shape[:3] + (2 * A_HEAD_DIM,)))
            dv_p.append(vp)
            dk_s.append(kn.reshape(kn.shape[:3] + (2 * A_HEAD_DIM,)))
            dv_s.append(vn)
        else:
            mix_p, c_p, pe_p = mla_prompt(hp, pos_p, b_w_a[j], b_q_norm[j], b_kv_norm[j], b_w_uq[j],
                                          b_w_ukv[j], b_w_o[j])
            mix_s, c_s, pe_s = mla_sample(hs, pos_s, page_table, mla_pages(cache_mla_ckv, cache_mla_kpe, j),
                                          b_w_a[j], b_q_norm[j], b_kv_norm[j], b_w_uq[j], b_w_ukv[j], b_w_o[j])
            mc_p.append(c_p)
            mpe_p.append(pe_p)
            mc_s.append(c_s)
            mpe_s.append(pe_s)
        xp = xp + mix_p.astype(xp.dtype)
        xs = xs + mix_s.astype(xs.dtype)
        xp = xp + sq_relu_mlp(rmsnorm(xp, norm_mlp[i]), w_up[i], w_down[i]).astype(xp.dtype)
        xs = xs + sq_relu_mlp(rmsnorm(xs, norm_mlp[i]), w_up[i], w_down[i]).astype(xs.dtype)

    y_prompt = rmsnorm(xp, norm_final)[:, N_META:]
    y_sample = rmsnorm(xs, norm_final)
    return (y_prompt, y_sample,
            jnp.stack(dk_p), jnp.stack(dv_p), jnp.stack(mc_p), jnp.stack(mpe_p),
            jnp.stack(dk_s), jnp.stack(dv_s), jnp.stack(mc_s), jnp.stack(mpe_s))
```

```python
import functools
import math

import jax
import jax.numpy as jnp
from jax import lax
from jax.experimental import pallas as pl
from jax.experimental.pallas import tpu as pltpu

F32 = jnp.float32
BF16 = jnp.bfloat16

RMS_EPS = 1e-6
ROPE_THETA = 10000.0
NEG_INF = -1e30
LANES = 128
ROPE_DIM = 64
ROPE_HALF = ROPE_DIM // 2

A_HEAD_DIM = 64
A_KV_HEADS = 4
A_GROUP = 2
A_MAPS = 2
A_SCALE = A_HEAD_DIM ** -0.5
B_HEADS = 8
B_NOPE = 128
B_ROPE = 64
B_V = 128
B_Q_LORA = 384
B_KV_LORA = 256
B_SCALE = (B_NOPE + B_ROPE) ** -0.5

VMEM_LIMIT_BYTES = 56 * 1024 * 1024
DECODE_PAGES_PER_STEP = 8


def _cparams(n_axes):
    return pltpu.CompilerParams(
        dimension_semantics=("arbitrary",) * n_axes,
        vmem_limit_bytes=VMEM_LIMIT_BYTES,
    )


def _resident(shape):
    nd = len(shape)
    return pl.BlockSpec(shape, lambda *_: (0,) * nd, pipeline_mode=pl.Buffered(1))


def _row_tile(n):
    for t in (512, 256, 128, 64, 32, 16, 8):
        if n % t == 0:
            return t
    raise ValueError(f"row count {n} is not a multiple of 8")


def _attn_tiles(length):
    if length > 2048:
        return 768, 256
    return 128, 128


def _rms(x, w):
    var = jnp.mean(x * x, axis=-1, keepdims=True)
    return x * lax.rsqrt(var + RMS_EPS) * w


def _rope_masks():
    lane = lax.broadcasted_iota(jnp.int32, (1, LANES), 1)
    return (lane % ROPE_DIM) < ROPE_HALF, lane < ROPE_DIM


def _rope128(x, cos, sin_signed, first_half):
    rot = jnp.where(first_half, pltpu.roll(x, LANES - ROPE_HALF, 1), pltpu.roll(x, ROPE_HALF, 1))
    return x * cos + rot * sin_signed


def _dot(a, b):
    return jnp.dot(a, b, preferred_element_type=F32)


def _dot_nt(a, b):
    return lax.dot_general(a, b, (((1,), (1,)), ((), ())), preferred_element_type=F32)


def _diff_qkv_kernel(x_ref, nw_ref, w_ref, cos_ref, sin_ref,
                     q_ref, kz_ref, vb_ref, kf_ref, vf_ref):
    h = _rms(x_ref[...], nw_ref[...]).astype(BF16)
    y = _dot(h, w_ref[...])
    cos = cos_ref[...]
    sin = sin_ref[...]
    first_half, low = _rope_masks()
    n_q = A_KV_HEADS * A_GROUP
    for hh in range(n_q):
        r = _rope128(y[:, hh * LANES:(hh + 1) * LANES], cos, sin, first_half)
        q_ref[hh] = (r * A_SCALE).astype(BF16)
    k0 = n_q * LANES
    v0 = k0 + A_KV_HEADS * LANES
    for g in range(A_KV_HEADS):
        r = _rope128(y[:, k0 + g * LANES:k0 + (g + 1) * LANES], cos, sin, first_half)
        kf_ref[:, g * LANES:(g + 1) * LANES] = r
        kz_ref[2 * g] = jnp.where(low, r, 0.0).astype(BF16)
        kz_ref[2 * g + 1] = jnp.where(low, 0.0, r).astype(BF16)
        v = y[:, v0 + g * LANES:v0 + (g + 1) * LANES]
        vf_ref[:, g * LANES:(g + 1) * LANES] = v
        vb_ref[g] = v.astype(BF16)


def _diff_qkv(x, nw, w_bf16, cos, sin):
    n, d = x.shape
    tm = _row_tile(n)
    n_q = A_KV_HEADS * A_GROUP
    kv_cols = A_KV_HEADS * LANES
    row = lambda i: (i, 0)
    head_row = lambda i: (0, i, 0)
    return pl.pallas_call(
        _diff_qkv_kernel,
        grid=(n // tm,),
        in_specs=[
            pl.BlockSpec((tm, d), row),
            _resident((1, d)),
            _resident(w_bf16.shape),
            pl.BlockSpec((tm, LANES), row),
            pl.BlockSpec((tm, LANES), row),
        ],
        out_specs=[
            pl.BlockSpec((n_q, tm, LANES), head_row),
            pl.BlockSpec((2 * A_KV_HEADS, tm, LANES), head_row),
            pl.BlockSpec((A_KV_HEADS, tm, LANES), head_row),
            pl.BlockSpec((tm, kv_cols), row),
            pl.BlockSpec((tm, kv_cols), row),
        ],
        out_shape=[
            jax.ShapeDtypeStruct((n_q, n, LANES), BF16),
            jax.ShapeDtypeStruct((2 * A_KV_HEADS, n, LANES), BF16),
            jax.ShapeDtypeStruct((A_KV_HEADS, n, LANES), BF16),
            jax.ShapeDtypeStruct((n, kv_cols), F32),
            jax.ShapeDtypeStruct((n, kv_cols), F32),
        ],
        compiler_params=_cparams(1),
        name="diff_qkv",
    )(x, nw, w_bf16, cos, sin)


def _softmax_update(s, m_prev, l_prev):
    m_new = jnp.maximum(m_prev, jnp.max(s, axis=-1, keepdims=True))
    alpha = jnp.exp(m_prev - m_new)
    p = jnp.exp(s - m_new)
    l_new = alpha * l_prev + jnp.sum(p, axis=-1, keepdims=True)
    return m_new, l_new, alpha, p


def _diff_lambda(lam_ref, lam_init):
    lp = lam_ref[...]
    e1 = jnp.exp(jnp.sum(lp[0:1] * lp[1:2], axis=-1, keepdims=True))
    e2 = jnp.exp(jnp.sum(lp[2:3] * lp[3:4], axis=-1, keepdims=True))
    return e1 - e2 + lam_init


def _prompt_attn_kernel(*refs, n_maps, stack, tq, tk, lam_init):
    if lam_init is None:
        q_ref, k_ref, v_ref, o_ref, m_ref, l_ref, acc_ref = refs
    else:
        q_ref, k_ref, v_ref, lam_ref, subln_ref, o_ref, m_ref, l_ref, acc_ref = refs
    qi = pl.program_id(2)
    rows = stack * tq
    q = q_ref[...].reshape(rows, q_ref.shape[-1])

    m_ref[...] = jnp.full(m_ref.shape, NEG_INF, F32)
    l_ref[...] = jnp.zeros(l_ref.shape, F32)
    acc_ref[...] = jnp.zeros(acc_ref.shape, F32)

    def step(kb, mask):
        v = v_ref[0, pl.ds(kb, tk), :]
        for mi in range(n_maps):
            k = k_ref[mi, pl.ds(kb, tk), :]
            s = _dot_nt(q, k)
            if mask is not None:
                s = jnp.where(mask, s, NEG_INF)
            m_new, l_new, alpha, p = _softmax_update(s, m_ref[mi], l_ref[mi])
            acc_ref[mi] = alpha * acc_ref[mi] + _dot(p.astype(BF16), v)
            m_ref[mi] = m_new
            l_ref[mi] = l_new

    def full_chunk(c, carry):
        step(pl.multiple_of(c * tk, tk), None)
        return carry

    per_q = tq // tk
    lax.fori_loop(0, qi * per_q, full_chunk, 0)
    row_pos = lax.broadcasted_iota(jnp.int32, (rows, tk), 0) % tq
    col_pos = lax.broadcasted_iota(jnp.int32, (rows, tk), 1)
    for c in range(per_q):
        step(pl.multiple_of(qi * tq + c * tk, tk), col_pos + c * tk <= row_pos)

    if lam_init is None:
        o_ref[...] = (acc_ref[0] / l_ref[0]).astype(o_ref.dtype)
    else:
        lam = _diff_lambda(lam_ref, lam_init)
        d = acc_ref[0] / l_ref[0] - lam * (acc_ref[1] / l_ref[1])
        a = _rms(d, subln_ref[...]) * (1.0 - lam_init)
        for r in range(stack):
            o_ref[:, r * LANES:(r + 1) * LANES] = a[r * tq:(r + 1) * tq].astype(o_ref.dtype)


def _diff_prompt_attn(q, kz, vb, lam_params, subln, batch, lp, lam_init):
    tq, tk = _attn_tiles(lp)
    nq = lp // tq
    n = batch * lp
    kern = functools.partial(_prompt_attn_kernel, n_maps=A_MAPS, stack=A_GROUP, tq=tq, tk=tk,
                             lam_init=lam_init)
    rows = A_GROUP * tq
    return pl.pallas_call(
        kern,
        grid=(batch, A_KV_HEADS, nq),
        in_specs=[
            pl.BlockSpec((A_GROUP, tq, LANES), lambda b, g, i: (g, b * nq + i, 0)),
            pl.BlockSpec((A_MAPS, lp, LANES), lambda b, g, i: (g, b, 0)),
            pl.BlockSpec((1, lp, LANES), lambda b, g, i: (g, b, 0)),
            pl.BlockSpec(lam_params.shape, lambda b, g, i: (0, 0)),
            pl.BlockSpec(subln.shape, lambda b, g, i: (0, 0)),
        ],
        out_specs=pl.BlockSpec((tq, A_GROUP * LANES), lambda b, g, i: (b * nq + i, g)),
        out_shape=jax.ShapeDtypeStruct((n, A_KV_HEADS * A_GROUP * LANES), BF16),
        scratch_shapes=[
            pltpu.VMEM((A_MAPS, rows, 1), F32),
            pltpu.VMEM((A_MAPS, rows, 1), F32),
            pltpu.VMEM((A_MAPS, rows, LANES), F32),
        ],
        compiler_params=_cparams(3),
        name="diff_prompt_attn",
    )(q, kz, vb, lam_params, subln)


def _mla_prompt_attn(q, k, v, batch, lp):
    tq, tk = _attn_tiles(lp)
    nq = lp // tq
    n = batch * lp
    dk = q.shape[-1]
    kern = functools.partial(_prompt_attn_kernel, n_maps=1, stack=1, tq=tq, tk=tk, lam_init=None)
    return pl.pallas_call(
        kern,
        grid=(batch, B_HEADS, nq),
        in_specs=[
            pl.BlockSpec((1, tq, dk), lambda b, h, i: (h, b * nq + i, 0)),
            pl.BlockSpec((1, lp, dk), lambda b, h, i: (h, b, 0)),
            pl.BlockSpec((1, lp, B_V), lambda b, h, i: (h, b, 0)),
        ],
        out_specs=pl.BlockSpec((tq, B_V), lambda b, h, i: (b * nq + i, h)),
        out_shape=jax.ShapeDtypeStruct((n, B_HEADS * B_V), BF16),
        scratch_shapes=[
            pltpu.VMEM((1, tq, 1), F32),
            pltpu.VMEM((1, tq, 1), F32),
            pltpu.VMEM((1, tq, B_V), F32),
        ],
        compiler_params=_cparams(3),
        name="mla_prompt_attn",
    )(q, k, v)


def _page_specs(block, layer_base, n_per_step):
    nd = len(block)

    def make(i):
        def index_map(s, c, pt_ref):
            return (layer_base + pt_ref[s, c * n_per_step + i],) + (0,) * (nd - 1)
        return pl.BlockSpec(block, index_map)

    return [make(i) for i in range(n_per_step)]


def _diff_decode_kernel(pt_ref, q_ref, kn_ref, vn_ref, lam_ref, subln_ref, *rest,
                        n_pages_step, t_new, lam_init):
    k_refs = rest[:n_pages_step]
    v_refs = rest[n_pages_step:2 * n_pages_step]
    o_ref, m_ref, l_ref, acc_ref = rest[2 * n_pages_step:]
    c = pl.program_id(1)
    q = q_ref[0]
    rows_g = A_MAPS * A_GROUP * t_new

    def attend(k, v, mask):
        s = jnp.concatenate(
            [_dot_nt(q[g * rows_g:(g + 1) * rows_g], k[:, g * LANES:(g + 1) * LANES])
             for g in range(A_KV_HEADS)], axis=0)
        if mask is not None:
            s = jnp.where(mask, s, NEG_INF)
        m_new, l_new, alpha, p = _softmax_update(s, m_ref[...], l_ref[...])
        pb = p.astype(BF16)
        pv = jnp.concatenate(
            [_dot(pb[g * rows_g:(g + 1) * rows_g], v[:, g * LANES:(g + 1) * LANES])
             for g in range(A_KV_HEADS)], axis=0)
        acc_ref[...] = alpha * acc_ref[...] + pv
        m_ref[...] = m_new
        l_ref[...] = l_new

    @pl.when(c == 0)
    def _():
        m_ref[...] = jnp.full(m_ref.shape, NEG_INF, F32)
        l_ref[...] = jnp.zeros(l_ref.shape, F32)
        acc_ref[...] = jnp.zeros(acc_ref.shape, F32)
        n_rows, n_keys = q.shape[0], kn_ref.shape[1]
        t_row = lax.broadcasted_iota(jnp.int32, (n_rows, n_keys), 0) % t_new
        t_key = lax.broadcasted_iota(jnp.int32, (n_rows, n_keys), 1)
        attend(kn_ref[0], vn_ref[0], t_key <= t_row)

    k = jnp.concatenate([r[0] for r in k_refs], axis=0).astype(BF16)
    v = jnp.concatenate([r[0] for r in v_refs], axis=0).astype(BF16)
    attend(k, v, None)

    @pl.when(c == pl.num_programs(1) - 1)
    def _():
        lam = _diff_lambda(lam_ref, lam_init)
        o = acc_ref[...] / l_ref[...]
        half = A_GROUP * t_new
        for g in range(A_KV_HEADS):
            for r in range(A_GROUP):
                r0 = g * rows_g + r * t_new
                d = o[r0:r0 + t_new] - lam * o[r0 + half:r0 + half + t_new]
                a = _rms(d, subln_ref[...]) * (1.0 - lam_init)
                hh = g * A_GROUP + r
                o_ref[0, :, hh * LANES:(hh + 1) * LANES] = a


def _diff_decode_attn(q, k_new, v_new, lam_params, subln, k_pool, v_pool, base, page_table, lam_init):
    n_seq, n_pages = page_table.shape
    pps = DECODE_PAGES_PER_STEP if n_pages % DECODE_PAGES_PER_STEP == 0 else 1
    t_new = q.shape[1] // (A_KV_HEADS * A_MAPS * A_GROUP)
    page, width = k_pool.shape[1], k_pool.shape[2]
    kern = functools.partial(_diff_decode_kernel, n_pages_step=pps, t_new=t_new, lam_init=lam_init)
    seq3 = lambda s, c, pt: (s, 0, 0)
    const2 = lambda s, c, pt: (0, 0)
    out_cols = A_KV_HEADS * A_GROUP * LANES
    grid_spec = pltpu.PrefetchScalarGridSpec(
        num_scalar_prefetch=1,
        grid=(n_seq, n_pages // pps),
        in_specs=[
            pl.BlockSpec((1,) + q.shape[1:], seq3),
            pl.BlockSpec((1,) + k_new.shape[1:], seq3),
            pl.BlockSpec((1,) + v_new.shape[1:], seq3),
            pl.BlockSpec(lam_params.shape, const2),
            pl.BlockSpec(subln.shape, const2),
        ] + _page_specs((1, page, width), base, pps) + _page_specs((1, page, width), base, pps),
        out_specs=pl.BlockSpec((1, t_new, out_cols), seq3),
        scratch_shapes=[
            pltpu.VMEM((q.shape[1], 1), F32),
            pltpu.VMEM((q.shape[1], 1), F32),
            pltpu.VMEM((q.shape[1], LANES), F32),
        ],
    )
    return pl.pallas_call(
        kern,
        grid_spec=grid_spec,
        out_shape=jax.ShapeDtypeStruct((n_seq, t_new, out_cols), F32),
        compiler_params=_cparams(2),
        name="diff_decode_attn",
    )(page_table, q, k_new, v_new, lam_params, subln, *([k_pool] * pps), *([v_pool] * pps))


def _mla_decode_kernel(pt_ref, ql_ref, qp_ref, cn_ref, pn_ref, wv_ref, *rest, n_pages_step, t_new):
    c_refs = rest[:n_pages_step]
    p_refs = rest[n_pages_step:2 * n_pages_step]
    o_ref, m_ref, l_ref, acc_ref = rest[2 * n_pages_step:]
    c = pl.program_id(1)
    q_lat = ql_ref[0]
    q_pe = qp_ref[0]

    def attend(ckv, kpe, mask):
        s = _dot_nt(q_lat, ckv) + _dot_nt(q_pe, kpe)
        if mask is not None:
            s = jnp.where(mask, s, NEG_INF)
        m_new, l_new, alpha, p = _softmax_update(s, m_ref[...], l_ref[...])
        acc_ref[...] = alpha * acc_ref[...] + _dot(p.astype(BF16), ckv)
        m_ref[...] = m_new
        l_ref[...] = l_new

    @pl.when(c == 0)
    def _():
        m_ref[...] = jnp.full(m_ref.shape, NEG_INF, F32)
        l_ref[...] = jnp.zeros(l_ref.shape, F32)
        acc_ref[...] = jnp.zeros(acc_ref.shape, F32)
        n_rows, n_keys = q_lat.shape[0], cn_ref.shape[1]
        t_row = lax.broadcasted_iota(jnp.int32, (n_rows, n_keys), 0) % t_new
        t_key = lax.broadcasted_iota(jnp.int32, (n_rows, n_keys), 1)
        attend(cn_ref[0], pn_ref[0], t_key <= t_row)

    ckv = jnp.concatenate([r[0] for r in c_refs], axis=0).astype(BF16)
    kpe = jnp.concatenate([r[0] for r in p_refs], axis=0).astype(BF16)
    attend(ckv, kpe, None)

    @pl.when(c == pl.num_programs(1) - 1)
    def _():
        o_lat = (acc_ref[...] / l_ref[...]).astype(BF16)
        for h in range(B_HEADS):
            o_ref[0, :, h * B_V:(h + 1) * B_V] = _dot(o_lat[h * t_new:(h + 1) * t_new], wv_ref[h])


def _mla_decode_attn(q_lat, q_pe, c_new, pe_new, w_v, c_pool, pe_pool, base, page_table):
    n_seq, n_pages = page_table.shape
    pps = DECODE_PAGES_PER_STEP if n_pages % DECODE_PAGES_PER_STEP == 0 else 1
    t_new = q_lat.shape[1] // B_HEADS
    page = c_pool.shape[1]
    kern = functools.partial(_mla_decode_kernel, n_pages_step=pps, t_new=t_new)
    seq3 = lambda s, c, pt: (s, 0, 0)
    out_cols = B_HEADS * B_V
    grid_spec = pltpu.PrefetchScalarGridSpec(
        num_scalar_prefetch=1,
        grid=(n_seq, n_pages // pps),
        in_specs=[
            pl.BlockSpec((1,) + q_lat.shape[1:], seq3),
            pl.BlockSpec((1,) + q_pe.shape[1:], seq3),
            pl.BlockSpec((1,) + c_new.shape[1:], seq3),
            pl.BlockSpec((1,) + pe_new.shape[1:], seq3),
            pl.BlockSpec(w_v.shape, lambda s, c, pt: (0, 0, 0)),
        ] + _page_specs((1, page, c_pool.shape[2]), base, pps)
          + _page_specs((1, page, pe_pool.shape[2]), base, pps),
        out_specs=pl.BlockSpec((1, t_new, out_cols), seq3),
        scratch_shapes=[
            pltpu.VMEM((q_lat.shape[1], 1), F32),
            pltpu.VMEM((q_lat.shape[1], 1), F32),
            pltpu.VMEM((q_lat.shape[1], c_pool.shape[2]), F32),
        ],
    )
    return pl.pallas_call(
        kern,
        grid_spec=grid_spec,
        out_shape=jax.ShapeDtypeStruct((n_seq, t_new, out_cols), F32),
        compiler_params=_cparams(2),
        name="mla_decode_attn",
    )(page_table, q_lat, q_pe, c_new, pe_new, w_v, *([c_pool] * pps), *([pe_pool] * pps))


def _mla_proj_kernel(x_ref, nw_ref, wa_ref, qn_ref, kvn_ref, wuq_ref, wk_ref, cos_ref, sin_ref,
                     *outs, sample):
    h = _rms(x_ref[...], nw_ref[...]).astype(BF16)
    y = _dot(h, wa_ref[...])
    cos = cos_ref[...]
    sin = sin_ref[...]
    first_half, low = _rope_masks()
    cq = _rms(y[:, :B_Q_LORA], qn_ref[...]).astype(BF16)
    q = _dot(cq, wuq_ref[...]) * B_SCALE
    ckv = _rms(y[:, B_Q_LORA:B_Q_LORA + B_KV_LORA], kvn_ref[...])
    pe0 = B_Q_LORA + B_KV_LORA
    kpe = _rope128(y[:, pe0:pe0 + LANES], cos, sin, first_half)
    nope_cols = B_HEADS * B_NOPE
    q_pe = [_rope128(q[:, nope_cols + j * LANES:nope_cols + (j + 1) * LANES], cos, sin, first_half)
            for j in range(B_HEADS // 2)]

    if sample:
        ql_ref, qp_ref, ckv_ref, kpe_ref = outs
        for hd in range(B_HEADS):
            q_nope = q[:, hd * B_NOPE:(hd + 1) * B_NOPE].astype(BF16)
            ql_ref[hd] = _dot(q_nope, wk_ref[hd]).astype(BF16)
        for j in range(B_HEADS // 2):
            qp_ref[:, j * LANES:(j + 1) * LANES] = q_pe[j].astype(BF16)
    else:
        qo_ref, ko_ref, vo_ref, ckv_ref, kpe_ref = outs
        kv = _dot(ckv.astype(BF16), wk_ref[...])
        kpe_hi = pltpu.roll(kpe, ROPE_DIM, 1)
        hw = B_NOPE + B_V
        for hd in range(B_HEADS):
            even = hd % 2 == 0
            pe = q_pe[hd // 2]
            qo_ref[hd, :, :B_NOPE] = q[:, hd * B_NOPE:(hd + 1) * B_NOPE].astype(BF16)
            qo_ref[hd, :, B_NOPE:] = (jnp.where(low, pe, 0.0) if even else jnp.where(low, 0.0, pe)).astype(BF16)
            ko_ref[hd, :, :B_NOPE] = kv[:, hd * hw:hd * hw + B_NOPE].astype(BF16)
            ko_ref[hd, :, B_NOPE:] = (kpe if even else kpe_hi).astype(BF16)
            vo_ref[hd] = kv[:, hd * hw + B_NOPE:(hd + 1) * hw].astype(BF16)
    ckv_ref[...] = ckv
    kpe_ref[...] = kpe[:, :B_ROPE]


def _mla_proj(x, nw, wa, qn, kvn, wuq, wk, cos, sin, sample):
    n, d = x.shape
    tm = _row_tile(n)
    row = lambda i: (i, 0)
    head_row = lambda i: (0, i, 0)
    dk = B_NOPE + LANES
    if sample:
        out_specs = [
            pl.BlockSpec((B_HEADS, tm, B_KV_LORA), head_row),
            pl.BlockSpec((tm, B_HEADS * B_ROPE), row),
        ]
        out_shape = [
            jax.ShapeDtypeStruct((B_HEADS, n, B_KV_LORA), BF16),
            jax.ShapeDtypeStruct((n, B_HEADS * B_ROPE), BF16),
        ]
    else:
        out_specs = [
            pl.BlockSpec((B_HEADS, tm, dk), head_row),
            pl.BlockSpec((B_HEADS, tm, dk), head_row),
            pl.BlockSpec((B_HEADS, tm, B_V), head_row),
        ]
        out_shape = [
            jax.ShapeDtypeStruct((B_HEADS, n, dk), BF16),
            jax.ShapeDtypeStruct((B_HEADS, n, dk), BF16),
            jax.ShapeDtypeStruct((B_HEADS, n, B_V), BF16),
        ]
    out_specs += [pl.BlockSpec((tm, B_KV_LORA), row), pl.BlockSpec((tm, B_ROPE), row)]
    out_shape += [jax.ShapeDtypeStruct((n, B_KV_LORA), F32), jax.ShapeDtypeStruct((n, B_ROPE), F32)]
    return pl.pallas_call(
        functools.partial(_mla_proj_kernel, sample=sample),
        grid=(n // tm,),
        in_specs=[
            pl.BlockSpec((tm, d), row),
            _resident((1, d)),
            _resident(wa.shape),
            _resident(qn.shape),
            _resident(kvn.shape),
            _resident(wuq.shape),
            _resident(wk.shape),
            pl.BlockSpec((tm, LANES), row),
            pl.BlockSpec((tm, LANES), row),
        ],
        out_specs=out_specs,
        out_shape=out_shape,
        compiler_params=_cparams(1),
        name="mla_proj_sample" if sample else "mla_proj_prompt",
    )(x, nw, wa, qn, kvn, wuq, wk, cos, sin)


def _proj_mlp_kernel(*refs, final, ff_chunk):
    if final:
        x_ref, a_ref, wo_ref, nw_ref, wup_ref, wdn_ref, nf_ref, o_ref = refs
    else:
        x_ref, a_ref, wo_ref, nw_ref, wup_ref, wdn_ref, o_ref = refs
    x = x_ref[...] + _dot(a_ref[...].astype(BF16), wo_ref[...])
    h = _rms(x, nw_ref[...]).astype(BF16)
    acc = x
    for c in range(wup_ref.shape[1] // ff_chunk):
        u = jnp.maximum(_dot(h, wup_ref[:, c * ff_chunk:(c + 1) * ff_chunk]), 0.0)
        acc = acc + _dot((u * u).astype(BF16), wdn_ref[c * ff_chunk:(c + 1) * ff_chunk, :])
    o_ref[...] = _rms(acc, nf_ref[...]) if final else acc


def _proj_mlp(x, a, wo, nw, wup, wdn, nf):
    n, d = x.shape
    tm = _row_tile(n)
    row = lambda i: (i, 0)
    final = nf is not None
    ins = [x, a, wo, nw, wup, wdn] + ([nf] if final else [])
    in_specs = [
        pl.BlockSpec((tm, d), row),
        pl.BlockSpec((tm, a.shape[1]), row),
        _resident(wo.shape),
        _resident(nw.shape),
        _resident(wup.shape),
        _resident(wdn.shape),
    ] + ([_resident(nf.shape)] if final else [])
    return pl.pallas_call(
        functools.partial(_proj_mlp_kernel, final=final, ff_chunk=min(1024, wup.shape[1])),
        grid=(n // tm,),
        in_specs=in_specs,
        out_specs=pl.BlockSpec((tm, d), row),
        out_shape=jax.ShapeDtypeStruct((n, d), F32),
        compiler_params=_cparams(1),
        name="proj_mlp_final" if final else "proj_mlp",
    )(*ins)


def _rope_tables(pos):
    inv_freq = 1.0 / (ROPE_THETA ** (jnp.arange(ROPE_HALF, dtype=F32) / ROPE_HALF))
    ang = pos.astype(F32)[:, None] * inv_freq[None, :]
    cos, sin = jnp.cos(ang), jnp.sin(ang)
    reps = LANES // ROPE_DIM
    return (jnp.concatenate([cos, cos] * reps, axis=1),
            jnp.concatenate([-sin, sin] * reps, axis=1))


def _pad_rows(x, rows):
    return jnp.pad(x, ((0, 0), (0, rows - x.shape[1]), (0, 0)))


def kernel(x_prompt, x_sample, cache_diff_k, cache_diff_v, cache_mla_ckv, cache_mla_kpe, page_table,
           meta_tokens, norm_mix, norm_mlp, norm_final, a_w_qkv, a_lambda, a_subln, a_w_o,
           b_w_a, b_q_norm, b_kv_norm, b_w_uq, b_w_ukv, b_w_o, w_up, w_down):
    batch, seq, d = x_prompt.shape
    n_meta = meta_tokens.shape[0]
    n_seq, t_new, _ = x_sample.shape
    depth = norm_mix.shape[0]
    n_pages = page_table.shape[1]
    page = cache_diff_k.shape[2]
    length = n_meta + seq
    tq, _ = _attn_tiles(length)
    lp = -(-length // tq) * tq
    n_s = n_seq * t_new

    meta = jnp.broadcast_to(meta_tokens[None].astype(x_prompt.dtype), (batch,) + meta_tokens.shape)
    xp = jnp.concatenate([meta, x_prompt, jnp.zeros((batch, lp - length, d), x_prompt.dtype)], axis=1)
    xp = xp.reshape(batch * lp, d)
    xs = x_sample.reshape(n_s, d)

    cos_p, sin_p = _rope_tables(jnp.tile(jnp.arange(lp), batch))
    cos_s, sin_s = _rope_tables(jnp.tile(n_pages * page + jnp.arange(t_new), n_seq))

    n_layers_a, n_pool = cache_diff_k.shape[:2]
    n_layers_b = cache_mla_ckv.shape[0]
    dk_pool = cache_diff_k.reshape(n_layers_a * n_pool, page, -1)
    dv_pool = cache_diff_v.reshape(n_layers_a * n_pool, page, -1)
    mc_pool = cache_mla_ckv.reshape(n_layers_b * n_pool, page, -1)
    mpe_pool = cache_mla_kpe.reshape(n_layers_b * n_pool, page, -1)

    def row2(v):
        return v.reshape(1, -1).astype(F32)

    dk_p, dv_p, dk_s, dv_s = [], [], [], []
    mc_p, mpe_p, mc_s, mpe_s = [], [], [], []
    y_p = y_s = None
    for i in range(depth):
        j = i // 2
        nw_mix = row2(norm_mix[i])
        if i % 2 == 0:
            lam_init = 0.8 - 0.6 * math.exp(-0.3 * i)
            w_qkv = a_w_qkv[j].astype(BF16)
            lam_params = a_lambda[j].astype(F32)
            subln = row2(a_subln[j])
            w_o = a_w_o[j].astype(BF16)

            q, kz, vb, kf, vf = _diff_qkv(xp, nw_mix, w_qkv, cos_p, sin_p)
            mix_p = _diff_prompt_attn(q, kz, vb, lam_params, subln, batch, lp, lam_init)
            dk_p.append(kf.reshape(batch, lp, A_KV_HEADS, LANES)[:, :length])
            dv_p.append(vf.reshape(batch, lp, A_KV_HEADS, LANES)[:, :length])

            q, _, _, kf, vf = _diff_qkv(xs, nw_mix, w_qkv, cos_s, sin_s)
            q = q.reshape(A_KV_HEADS, A_GROUP, n_seq, t_new, LANES).transpose(2, 0, 1, 3, 4)
            low = jnp.arange(LANES) < A_HEAD_DIM
            zero = jnp.zeros((), q.dtype)
            q = jnp.stack([jnp.where(low, q, zero), jnp.where(low, zero, q)], axis=2)
            q = q.reshape(n_seq, A_KV_HEADS * A_MAPS * A_GROUP * t_new, LANES)
            k_new = _pad_rows(kf.reshape(n_seq, t_new, -1).astype(BF16), page)
            v_new = _pad_rows(vf.reshape(n_seq, t_new, -1).astype(BF16), page)
            mix_s = _diff_decode_attn(q, k_new, v_new, lam_params, subln, dk_pool, dv_pool,
                                      j * n_pool, page_table, lam_init)
            mix_s = mix_s.reshape(n_s, -1)
            dk_s.append(kf.reshape(n_seq, t_new, A_KV_HEADS, LANES))
            dv_s.append(vf.reshape(n_seq, t_new, A_KV_HEADS, LANES))
        else:
            w_a = jnp.pad(b_w_a[j], ((0, 0), (0, B_Q_LORA + B_KV_LORA + LANES - b_w_a.shape[2]))).astype(BF16)
            qn, kvn = row2(b_q_norm[j]), row2(b_kv_norm[j])
            w_uq = b_w_uq[j].reshape(B_Q_LORA, B_HEADS, B_NOPE + B_ROPE)
            w_uq = jnp.concatenate([w_uq[:, :, :B_NOPE].reshape(B_Q_LORA, -1),
                                    w_uq[:, :, B_NOPE:].reshape(B_Q_LORA, -1)], axis=1).astype(BF16)
            w_ukv = b_w_ukv[j]
            w_kv = w_ukv.reshape(B_KV_LORA, -1).astype(BF16)
            w_kt = jnp.transpose(w_ukv[:, :, :B_NOPE], (1, 2, 0)).astype(BF16)
            w_v = jnp.transpose(w_ukv[:, :, B_NOPE:], (1, 0, 2)).astype(BF16)
            w_o = b_w_o[j].astype(BF16)

            q, k, v, ckv, kpe = _mla_proj(xp, nw_mix, w_a, qn, kvn, w_uq, w_kv, cos_p, sin_p, sample=False)
            mix_p = _mla_prompt_attn(q, k, v, batch, lp)
            mc_p.append(ckv.reshape(batch, lp, -1)[:, :length])
            mpe_p.append(kpe.reshape(batch, lp, -1)[:, :length])

            q_lat, q_pe, ckv, kpe = _mla_proj(xs, nw_mix, w_a, qn, kvn, w_uq, w_kt, cos_s, sin_s, sample=True)
            q_lat = q_lat.reshape(B_HEADS, n_seq, t_new, -1).transpose(1, 0, 2, 3).reshape(n_seq, B_HEADS * t_new, -1)
            q_pe = q_pe.reshape(n_seq, t_new, B_HEADS, B_ROPE).transpose(0, 2, 1, 3).reshape(n_seq, B_HEADS * t_new, B_ROPE)
            c_new = _pad_rows(ckv.reshape(n_seq, t_new, -1).astype(BF16), page)
            pe_new = _pad_rows(kpe.reshape(n_seq, t_new, -1).astype(BF16), page)
            mix_s = _mla_decode_attn(q_lat, q_pe, c_new, pe_new, w_v, mc_pool, mpe_pool, j * n_pool, page_table)
            mix_s = mix_s.reshape(n_s, -1)
            mc_s.append(ckv.reshape(n_seq, t_new, -1))
            mpe_s.append(kpe.reshape(n_seq, t_new, -1))

        nf = row2(norm_final) if i == depth - 1 else None
        wup, wdn = w_up[i].astype(BF16), w_down[i].astype(BF16)
        nw_mlp = row2(norm_mlp[i])
        xp = _proj_mlp(xp, mix_p, w_o, nw_mlp, wup, wdn, nf)
        xs = _proj_mlp(xs, mix_s, w_o, nw_mlp, wup, wdn, nf)

    y_prompt = xp.reshape(batch, lp, d)[:, n_meta:length]
    y_sample = xs.reshape(n_seq, t_new, d)
    return (y_prompt, y_sample,
            jnp.stack(dk_p), jnp.stack(dv_p), jnp.stack(mc_p), jnp.stack(mpe_p),
            jnp.stack(dk_s), jnp.stack(dv_s), jnp.stack(mc_s), jnp.stack(mpe_s))
```
